```python
import jax, jax.numpy as jnp
from jax import lax
import numpy as np

D_MODEL = 1024
BATCH = 4
SEQ = 4096
DEPTH = 4
DEC_BATCH = 16
DEC_SEQ = 2048
PAST_LEN = 128

N_META = 16
GRID_W = 64
HEAD_DIM = 64
N_Q_HEADS = 16
N_KV_HEADS = 4
Q_PER_KV = N_Q_HEADS // N_KV_HEADS
ATTN_WIDTH = N_Q_HEADS * HEAD_DIM
KV_WIDTH = N_KV_HEADS * HEAD_DIM
CONV_WIDTH = D_MODEL
CONV_K = 3
D_FF = 2816
Q_BLOCK = 128
ROPE_BASE = 10000.0
NORM_EPS = 1e-6
ROPE_HALF = HEAD_DIM // 2
IN_WIDTH = ATTN_WIDTH + 2 * KV_WIDTH + 3 * CONV_WIDTH + 2 * D_MODEL

kernel_name = "hybrid_gqa_shortconv_macaron_encoder"


def rmsnorm(x, g):
    xf = x.astype(jnp.float32)
    y = xf * lax.rsqrt(jnp.mean(xf * xf, axis=-1, keepdims=True) + NORM_EPS)
    return (y * g.astype(jnp.float32)).astype(x.dtype)


def swiglu(x, w_gate, w_up, w_down):
    return (jax.nn.silu(x @ w_gate) * (x @ w_up)) @ w_down


def axial_angles(n_tok):
    rows = n_tok // GRID_W
    row = jnp.repeat(jnp.arange(rows, dtype=jnp.float32), GRID_W)
    col = jnp.tile(jnp.arange(GRID_W, dtype=jnp.float32), rows)
    meta_row = jnp.full((N_META,), -1.0, dtype=jnp.float32)
    meta_col = jnp.arange(N_META, dtype=jnp.float32)
    row = jnp.concatenate([meta_row, row])
    col = jnp.concatenate([meta_col, col])
    inv_freq = ROPE_BASE ** (-jnp.arange(0, ROPE_HALF, 2, dtype=jnp.float32) / ROPE_HALF)
    ang_r = row[:, None] * inv_freq[None, :]
    ang_c = col[:, None] * inv_freq[None, :]
    return jnp.cos(ang_r), jnp.sin(ang_r), jnp.cos(ang_c), jnp.sin(ang_c)


def rope_rotate(x, cos, sin):
    x1, x2 = jnp.split(x, 2, axis=-1)
    c = cos[None, :, None, :]
    s = sin[None, :, None, :]
    return jnp.concatenate([x1 * c - x2 * s, x2 * c + x1 * s], axis=-1)


def axial_rope(x, angles):
    cos_r, sin_r, cos_c, sin_c = angles
    xf = x.astype(jnp.float32)
    out = jnp.concatenate([rope_rotate(xf[..., :ROPE_HALF], cos_r, sin_r),
                           rope_rotate(xf[..., ROPE_HALF:], cos_c, sin_c)], axis=-1)
    return out.astype(x.dtype)


def blocked_gqa(q, k, v):
    b, l = q.shape[0], q.shape[1]

    def attend(qb):
        s = jnp.einsum('bqkgd,bskd->bkgqs', qb, k).astype(jnp.float32)
        p = jax.nn.softmax(s, axis=-1).astype(v.dtype)
        return jnp.einsum('bkgqs,bskd->bqkgd', p, v)

    meta_out = attend(q[:, :N_META])
    real = q[:, N_META:]
    n_blk = real.shape[1] // Q_BLOCK
    blocks = real.reshape(b, n_blk, Q_BLOCK, N_KV_HEADS, Q_PER_KV, HEAD_DIM).transpose(1, 0, 2, 3, 4, 5)
    out = lax.map(attend, blocks)
    out = out.transpose(1, 0, 2, 3, 4, 5).reshape(b, n_blk * Q_BLOCK, N_KV_HEADS, Q_PER_KV, HEAD_DIM)
    return jnp.concatenate([meta_out, out], axis=1).reshape(b, l, ATTN_WIDTH)


def centred_dwconv(x, w, bias):
    xp = jnp.pad(x, ((0, 0), (1, 1), (0, 0)))
    return w[0] * xp[:, :-2] + w[1] * xp[:, 1:-1] + w[2] * xp[:, 2:] + bias


def token_mixer(u, angles, w_in, conv_w, conv_b, q_norm, k_norm, w_o_attn, w_o_conv, w_merge):
    b, l, _ = u.shape
    p = u @ w_in
    splits = np.cumsum([ATTN_WIDTH, KV_WIDTH, KV_WIDTH, CONV_WIDTH, CONV_WIDTH, CONV_WIDTH, D_MODEL])
    q, k, v, cb, cc, cx, g_attn, g_conv = jnp.split(p, splits, axis=-1)
    q = q.reshape(b, l, N_Q_HEADS, HEAD_DIM)
    k = k.reshape(b, l, N_KV_HEADS, HEAD_DIM)
    v = v.reshape(b, l, N_KV_HEADS, HEAD_DIM)
    q = axial_rope(rmsnorm(q, q_norm), angles) * (HEAD_DIM ** -0.5)
    k = axial_rope(rmsnorm(k, k_norm), angles)
    q = q.reshape(b, l, N_KV_HEADS, Q_PER_KV, HEAD_DIM)
    a = blocked_gqa(q, k, v) @ w_o_attn
    c = (cb * centred_dwconv(cc * cx, conv_w, conv_b)) @ w_o_conv
    merged = jax.nn.sigmoid(g_attn) * a + jax.nn.sigmoid(g_conv) * c
    return merged @ w_merge


def trunk(x, meta_tokens, ffn1_norm, ffn1_w_gate, ffn1_w_up, ffn1_w_down, mix_norm, w_in, conv_w,
          conv_b, q_norm, k_norm, w_o_attn, w_o_conv, w_merge, ffn2_norm, ffn2_w_gate, ffn2_w_up,
          ffn2_w_down, final_norm):
    b, n_tok, _ = x.shape
    meta = jnp.broadcast_to(meta_tokens.astype(x.dtype)[None], (b, N_META, D_MODEL))
    h = jnp.concatenate([meta, x], axis=1)
    angles = axial_angles(n_tok)
    for i in range(DEPTH):
        h = h + 0.5 * swiglu(rmsnorm(h, ffn1_norm[i]), ffn1_w_gate[i], ffn1_w_up[i], ffn1_w_down[i])
        h = h + token_mixer(rmsnorm(h, mix_norm[i]), angles, w_in[i], conv_w[i], conv_b[i], q_norm[i],
                            k_norm[i], w_o_attn[i], w_o_conv[i], w_merge[i])
        h = h + 0.5 * swiglu(rmsnorm(h, ffn2_norm[i]), ffn2_w_gate[i], ffn2_w_up[i], ffn2_w_down[i])
    h = rmsnorm(h, final_norm)
    return h[:, N_META:]


def setup_inputs(seed: int = 0) -> dict:
    key = jax.random.key(seed)
    ks = jax.random.split(key, 24)

    def nrm(k, shape, scale):
        return jax.random.normal(k, shape, dtype=jnp.float32) * scale

    def gain(k, shape):
        return 1.0 + nrm(k, shape, 0.01)

    return {
        "x_prompt": nrm(ks[0], (BATCH, SEQ, D_MODEL), 1.0),
        "x_sample": nrm(ks[1], (DEC_BATCH, DEC_SEQ, D_MODEL), 1.0),
        "meta_tokens": nrm(ks[2], (N_META, D_MODEL), 1.0),
        "ffn1_norm": gain(ks[3], (DEPTH, D_MODEL)),
        "ffn1_w_gate": nrm(ks[4], (DEPTH, D_MODEL, D_FF), D_MODEL ** -0.5),
        "ffn1_w_up": nrm(ks[5], (DEPTH, D_MODEL, D_FF), D_MODEL ** -0.5),
        "ffn1_w_down": nrm(ks[6], (DEPTH, D_FF, D_MODEL), D_FF ** -0.5),
        "mix_norm": gain(ks[7], (DEPTH, D_MODEL)),
        "w_in": nrm(ks[8], (DEPTH, D_MODEL, IN_WIDTH), D_MODEL ** -0.5),
        "conv_w": nrm(ks[9], (DEPTH, CONV_K, CONV_WIDTH), CONV_K ** -0.5),
        "conv_b": nrm(ks[10], (DEPTH, CONV_WIDTH), 0.01),
        "q_norm": gain(ks[11], (DEPTH, HEAD_DIM)),
        "k_norm": gain(ks[12], (DEPTH, HEAD_DIM)),
        "w_o_attn": nrm(ks[13], (DEPTH, ATTN_WIDTH, D_MODEL), ATTN_WIDTH ** -0.5),
        "w_o_conv": nrm(ks[14], (DEPTH, CONV_WIDTH, D_MODEL), CONV_WIDTH ** -0.5),
        "w_merge": nrm(ks[15], (DEPTH, D_MODEL, D_MODEL), D_MODEL ** -0.5),
        "ffn2_norm": gain(ks[16], (DEPTH, D_MODEL)),
        "ffn2_w_gate": nrm(ks[17], (DEPTH, D_MODEL, D_FF), D_MODEL ** -0.5),
        "ffn2_w_up": nrm(ks[18], (DEPTH, D_MODEL, D_FF), D_MODEL ** -0.5),
        "ffn2_w_down": nrm(ks[19], (DEPTH, D_FF, D_MODEL), D_FF ** -0.5),
        "final_norm": gain(ks[20], (D_MODEL,)),
    }


def reference(x_prompt, x_sample, meta_tokens, ffn1_norm, ffn1_w_gate, ffn1_w_up, ffn1_w_down, mix_norm,
              w_in, conv_w, conv_b, q_norm, k_norm, w_o_attn, w_o_conv, w_merge, ffn2_norm, ffn2_w_gate,
              ffn2_w_up, ffn2_w_down, final_norm):
    y_prompt = trunk(x_prompt, meta_tokens, ffn1_norm, ffn1_w_gate, ffn1_w_up, ffn1_w_down, mix_norm, w_in,
                     conv_w, conv_b, q_norm, k_norm, w_o_attn, w_o_conv, w_merge, ffn2_norm, ffn2_w_gate,
                     ffn2_w_up, ffn2_w_down, final_norm)
    y_sample = trunk(x_sample, meta_tokens, ffn1_norm, ffn1_w_gate, ffn1_w_up, ffn1_w_down, mix_norm, w_in,
                     conv_w, conv_b, q_norm, k_norm, w_o_attn, w_o_conv, w_merge, ffn2_norm, ffn2_w_gate,
                     ffn2_w_up, ffn2_w_down, final_norm)
    return (y_prompt, y_sample)
```

```python
import functools
from typing import NamedTuple

import jax
import jax.numpy as jnp
from jax import lax
from jax.experimental import pallas as pl
from jax.experimental.pallas import tpu as pltpu

F32 = jnp.float32
BF16 = jnp.bfloat16

GRID_W = 64
ROPE_BASE = 10000.0
NORM_EPS = 1e-6

LANES = 128
MXU_DIM = 256
BF16_SUBLANES = 16
VMEM_LIMIT_BYTES = 56 * 1024 * 1024


class Layout(NamedTuple):
    d: int
    hd: int
    nq: int
    nkv: int
    n_meta: int
    groups: tuple
    real: int
    total: int
    tm: int
    nb: int


def _pick_tile(sizes, candidates):
    for c in candidates:
        if all(s % c == 0 for s in sizes):
            return c
    raise ValueError(f"no tile in {candidates} divides {sizes}")


def _make_layout(x_groups, meta_tokens, q_norm, w_o_attn, w_in):
    d = x_groups[0].shape[-1]
    hd = q_norm.shape[-1]
    attn_w = w_o_attn.shape[1]
    kv_w = (w_in.shape[-1] - attn_w - 5 * d) // 2
    groups = tuple((x.shape[0], x.shape[1]) for x in x_groups)
    n_meta = meta_tokens.shape[0]
    nb = sum(b for b, _ in groups)
    real = sum(b * n for b, n in groups)
    tm = _pick_tile([n for _, n in groups], (512, 256, 128))
    assert n_meta == BF16_SUBLANES and nb * n_meta <= tm and LANES % n_meta == 0
    assert all(n % GRID_W == 0 for _, n in groups)
    return Layout(d=d, hd=hd, nq=attn_w // hd, nkv=kv_w // hd, n_meta=n_meta, groups=groups,
                  real=real, total=real + tm, tm=tm, nb=nb)


def _rmsnorm_rows(x, g):
    return x * lax.rsqrt(jnp.mean(x * x, axis=-1, keepdims=True) + NORM_EPS) * g


def _resident(block_shape, index_map):
    return pl.BlockSpec(block_shape, index_map, pipeline_mode=pl.Buffered(1))


def _params(*semantics):
    return pltpu.CompilerParams(dimension_semantics=semantics, vmem_limit_bytes=VMEM_LIMIT_BYTES)


def _ffn_kernel(x_ref, g_ref, wg_ref, wu_ref, wd_ref, o_ref, *, chunks):
    x = x_ref[...]
    u = _rmsnorm_rows(x, g_ref[...]).astype(BF16)
    acc = None
    for lo, hi in chunks:
        gate = jnp.dot(u, wg_ref[:, lo:hi], preferred_element_type=F32)
        up = jnp.dot(u, wu_ref[:, lo:hi], preferred_element_type=F32)
        act = (gate * jax.nn.sigmoid(gate) * up).astype(BF16)
        part = jnp.dot(act, wd_ref[lo:hi, :], preferred_element_type=F32)
        acc = part if acc is None else acc + part
    o_ref[...] = x + 0.5 * acc


def _ffn(lay, layer, x, norm, wg, wu, wd):
    d, tm = lay.d, lay.tm
    dff = wg.shape[-1]
    split = (dff // MXU_DIM + 1) // 2 * MXU_DIM
    chunks = ((0, split), (split, dff)) if 0 < split < dff else ((0, dff),)
    return pl.pallas_call(
        functools.partial(_ffn_kernel, chunks=chunks),
        out_shape=jax.ShapeDtypeStruct(x.shape, F32),
        grid=(lay.total // tm,),
        in_specs=[
            pl.BlockSpec((tm, d), lambda i: (i, 0)),
            _resident((None, 1, d), lambda i: (layer, 0, 0)),
            _resident((None, d, dff), lambda i: (layer, 0, 0)),
            _resident((None, d, dff), lambda i: (layer, 0, 0)),
            _resident((None, dff, d), lambda i: (layer, 0, 0)),
        ],
        out_specs=pl.BlockSpec((tm, d), lambda i: (i, 0)),
        compiler_params=_params("parallel"),
        name="ffn",
    )(x, norm, wg, wu, wd)


def _inproj_kernel(x_ref, g_ref, wqkv_ref, wrest_ref, qn_ref, kn_ref, rope_ref,
                   qt_ref, k_ref, vt_ref, cb_ref, z_ref, ga_ref, gc_ref, *, nq, nkv, hd, d):
    x = x_ref[...]
    u = _rmsnorm_rows(x, g_ref[...]).astype(BF16)
    pt = lax.dot_general(wqkv_ref[...], u, (((1,), (1,)), ((), ())), preferred_element_type=F32)
    quarter = hd // 4
    rope = rope_ref[...]
    cos_r, sin_r = rope[0:quarter], rope[quarter:2 * quarter]
    cos_c, sin_c = rope[2 * quarter:3 * quarter], rope[3 * quarter:]

    def norm_rope(t, gain, scale):
        y = t * lax.rsqrt(jnp.mean(t * t, axis=0, keepdims=True) + NORM_EPS) * gain
        x1r, x2r = y[0:quarter], y[quarter:2 * quarter]
        x1c, x2c = y[2 * quarter:3 * quarter], y[3 * quarter:]
        out = jnp.concatenate([x1r * cos_r - x2r * sin_r, x2r * cos_r + x1r * sin_r,
                               x1c * cos_c - x2c * sin_c, x2c * cos_c + x1c * sin_c], axis=0)
        return out * scale

    for h in range(nq):
        qt_ref[h * hd:(h + 1) * hd, :] = norm_rope(pt[h * hd:(h + 1) * hd], qn_ref[...], hd ** -0.5).astype(BF16)
    kt = jnp.concatenate([norm_rope(pt[(nq + h) * hd:(nq + h + 1) * hd], kn_ref[...], 1.0)
                          for h in range(nkv)], axis=0)
    k_tok = kt.T
    for h in range(nkv):
        k_ref[h] = k_tok[:, h * hd:(h + 1) * hd].astype(BF16)
    vt_ref[...] = pt[(nq + nkv) * hd:].astype(BF16)

    rest = jnp.dot(u, wrest_ref[...], preferred_element_type=F32)
    cb_ref[...] = rest[:, 0:d].astype(BF16)
    z_ref[...] = (rest[:, d:2 * d] * rest[:, 2 * d:3 * d]).astype(BF16)
    ga_ref[...] = jax.nn.sigmoid(rest[:, 3 * d:4 * d]).astype(BF16)
    gc_ref[...] = jax.nn.sigmoid(rest[:, 4 * d:5 * d]).astype(BF16)


def _inproj(lay, layer, x, norm, wqkv_t, wrest, qn, kn, rope):
    d, tm, hd, nq, nkv, total = lay.d, lay.tm, lay.hd, lay.nq, lay.nkv, lay.total
    qkv_w = (nq + 2 * nkv) * hd
    tok = lambda i: (i, 0)
    feat = lambda i: (0, i)
    return pl.pallas_call(
        functools.partial(_inproj_kernel, nq=nq, nkv=nkv, hd=hd, d=d),
        out_shape=(
            jax.ShapeDtypeStruct((nq * hd, total), BF16),
            jax.ShapeDtypeStruct((nkv, total, hd), BF16),
            jax.ShapeDtypeStruct((nkv * hd, total), BF16),
            jax.ShapeDtypeStruct((total, d), BF16),
            jax.ShapeDtypeStruct((total, d), BF16),
            jax.ShapeDtypeStruct((total, d), BF16),
            jax.ShapeDtypeStruct((total, d), BF16),
        ),
        grid=(total // tm,),
        in_specs=[
            pl.BlockSpec((tm, d), tok),
            _resident((None, 1, d), lambda i: (layer, 0, 0)),
            _resident((None, qkv_w, d), lambda i: (layer, 0, 0)),
            _resident((None, d, 5 * d), lambda i: (layer, 0, 0)),
            _resident((None, hd, 1), lambda i: (layer, 0, 0)),
            _resident((None, hd, 1), lambda i: (layer, 0, 0)),
            pl.BlockSpec((hd, tm), feat),
        ],
        out_specs=(
            pl.BlockSpec((nq * hd, tm), feat),
            pl.BlockSpec((nkv, tm, hd), lambda i: (0, i, 0)),
            pl.BlockSpec((nkv * hd, tm), feat),
            pl.BlockSpec((tm, d), tok),
            pl.BlockSpec((tm, d), tok),
            pl.BlockSpec((tm, d), tok),
            pl.BlockSpec((tm, d), tok),
        ),
        compiler_params=_params("parallel"),
        name="inproj",
    )(x, norm, wqkv_t, wrest, qn, kn, rope)


def _attn_kernel(*refs, gb0, n_tok, tk, hd, group, n_meta, meta_queries):
    q_ref, km_ref, k_ref, vmt_ref, vt_ref = refs[:5]
    o_ref, osc_ref, osc_t_ref = refs[-3:]
    tq = q_ref.shape[1]
    gb = gb0 + pl.program_id(0)
    slot = gb % (LANES // n_meta)
    key_row = lax.broadcasted_iota(jnp.int32, (LANES, tq), 0)
    own_meta = (key_row // n_meta) == slot

    def one_head(h, carry):
        rows = pl.ds(pl.multiple_of(h * hd, hd), hd)
        qh = q_ref[rows, :]
        s = jnp.dot(km_ref[...], qh, preferred_element_type=F32)
        s = jnp.where(own_meta, s, -jnp.inf)
        m = jnp.max(s, axis=0, keepdims=True)
        p = jnp.exp(s - m)
        l = jnp.sum(p, axis=0, keepdims=True)
        acc = jnp.dot(vmt_ref[...], p.astype(BF16), preferred_element_type=F32)
        for j in range(n_tok // tk):
            s = jnp.dot(k_ref[j * tk:(j + 1) * tk, :], qh, preferred_element_type=F32)
            m_new = jnp.maximum(m, jnp.max(s, axis=0, keepdims=True))
            alpha = jnp.exp(m - m_new)
            p = jnp.exp(s - m_new)
            l = alpha * l + jnp.sum(p, axis=0, keepdims=True)
            acc = alpha * acc + jnp.dot(vt_ref[:, j * tk:(j + 1) * tk], p.astype(BF16),
                                        preferred_element_type=F32)
            m = m_new
        osc_ref[rows, :] = acc / l
        return carry

    lax.fori_loop(0, group, one_head, 0)
    out = osc_ref[...].T
    if meta_queries:
        osc_t_ref[...] = out
        o_ref[...] = osc_t_ref[pl.ds(pl.multiple_of(slot * n_meta, n_meta), n_meta), :].astype(BF16)
    else:
        o_ref[...] = out.astype(BF16)


def _attention(lay, qt, k, vt, a_prev, *, row0, gb0, batch, n_tok, meta_queries):
    hd, nkv, nq = lay.hd, lay.nkv, lay.nq
    group = nq // nkv
    tk = _pick_tile([n_tok], (512, 256, 128))
    meta_blk0 = lay.real // LANES
    per = LANES // lay.n_meta
    assert row0 % n_tok == 0 and lay.real % LANES == 0
    if meta_queries:
        tq = LANES
        grid = (batch, nkv)
        q_map = lambda b, h: (h, meta_blk0 + (gb0 + b) // per)
        o_block = (lay.n_meta, group * hd)
        o_map = lambda b, h: (lay.real // lay.n_meta + gb0 + b, h)
        kv_args = lambda f: f
        semantics = ("parallel", "parallel")
    else:
        tq = _pick_tile([n_tok], (512, 256, 128))
        grid = (batch, nkv, n_tok // tq)
        q_map = lambda b, h, i: (h, (row0 + b * n_tok) // tq + i)
        o_block = (tq, group * hd)
        o_map = lambda b, h, i: ((row0 + b * n_tok) // tq + i, h)
        kv_args = lambda f: (lambda b, h, i: f(b, h))
        semantics = ("parallel", "parallel", "arbitrary")
    in_specs = [
        pl.BlockSpec((group * hd, tq), q_map),
        pl.BlockSpec((None, LANES, hd), kv_args(lambda b, h: (h, meta_blk0 + (gb0 + b) // per, 0))),
        pl.BlockSpec((None, n_tok, hd), kv_args(lambda b, h: (h, row0 // n_tok + b, 0))),
        pl.BlockSpec((hd, LANES), kv_args(lambda b, h: (h, meta_blk0 + (gb0 + b) // per))),
        pl.BlockSpec((hd, n_tok), kv_args(lambda b, h: (h, row0 // n_tok + b))),
    ]
    args = [qt, k, k, vt, vt]
    aliases = {}
    if a_prev is not None:
        in_specs.append(pl.BlockSpec(memory_space=pl.ANY))
        args.append(a_prev)
        aliases = {5: 0}
    return pl.pallas_call(
        functools.partial(_attn_kernel, gb0=gb0, n_tok=n_tok, tk=tk, hd=hd, group=group,
                          n_meta=lay.n_meta, meta_queries=meta_queries),
        out_shape=jax.ShapeDtypeStruct((lay.total, nq * hd), BF16),
        grid=grid,
        in_specs=in_specs,
        out_specs=pl.BlockSpec(o_block, o_map),
        scratch_shapes=[pltpu.VMEM((group * hd, tq), F32),
                        pltpu.VMEM((tq, group * hd) if meta_queries else (8, LANES), F32)],
        input_output_aliases=aliases,
        compiler_params=_params(*semantics),
        name="attn_meta" if meta_queries else "attn",
    )(*args)


def _merge_kernel(h_ref, a_ref, cb_ref, z_ref, zl_ref, zr_ref, ga_ref, gc_ref, cw_ref, cbias_ref,
                  woa_ref, woc_ref, wm_ref, o_ref, *, edge_fn):
    has_left, has_right = edge_fn(pl.program_id(0))
    z = z_ref[...].astype(F32)
    tm = z.shape[0]
    halo = zl_ref.shape[0]
    left = jnp.where(has_left, zl_ref[halo - 1:halo, :].astype(F32), 0.0)
    right = jnp.where(has_right, zr_ref[0:1, :].astype(F32), 0.0)
    row = lax.broadcasted_iota(jnp.int32, z.shape, 0)
    z_prev = jnp.where(row == 0, left, pltpu.roll(z, 1, axis=0))
    z_next = jnp.where(row == tm - 1, right, pltpu.roll(z, tm - 1, axis=0))
    cw = cw_ref[...]
    conv = cw[0:1] * z_prev + cw[1:2] * z + cw[2:3] * z_next + cbias_ref[...]
    conv_in = (cb_ref[...].astype(F32) * conv).astype(BF16)
    conv_out = jnp.dot(conv_in, woc_ref[...], preferred_element_type=F32)
    attn_out = jnp.dot(a_ref[...], woa_ref[...], preferred_element_type=F32)
    merged = (ga_ref[...].astype(F32) * attn_out + gc_ref[...].astype(F32) * conv_out).astype(BF16)
    o_ref[...] = h_ref[...] + jnp.dot(merged, wm_ref[...], preferred_element_type=F32)


def _merge(lay, layer, h, a, cb, z, ga, gc, conv_w, conv_b, woa, woc, wm, *, meta):
    d, tm = lay.d, lay.tm
    halo = BF16_SUBLANES
    meta_blk0 = lay.real // halo
    if meta:
        rows = lay.n_meta
        grid = (lay.nb,)
        starts = []
        row0 = 0
        for b, n in lay.groups:
            starts += [row0 + i * n for i in range(b)]
            row0 += b * n
        first_blk = jnp.asarray([s // halo for s in starts], jnp.int32)
        tile_map = lambda i, fb: (meta_blk0 + i, 0)
        left_map = lambda i, fb: (meta_blk0 + i, 0)
        right_map = lambda i, fb: (fb[i], 0)
        edge_fn = lambda i: (False, True)
        w_map = lambda i, fb: (layer, 0, 0)
        prefetch = (first_blk,)
    else:
        rows = tm
        grid = (lay.real // tm,)
        bounds = []
        t0, gb0 = 0, 0
        for b, n in lay.groups:
            bounds.append((t0, n // tm, gb0))
            t0 += b * n // tm
            gb0 += b

        def locate(i):
            base, per, g0 = bounds[0]
            for t_first, t_per, g_first in bounds[1:]:
                later = i >= t_first
                base = jnp.where(later, t_first, base)
                per = jnp.where(later, t_per, per)
                g0 = jnp.where(later, g_first, g0)
            j = i - base
            return j % per == 0, j % per == per - 1, g0 + j // per

        def left_map(i):
            first, _, gb = locate(i)
            return (jnp.where(first, meta_blk0 + gb, i * (tm // halo) - 1), 0)

        def right_map(i):
            _, last, _ = locate(i)
            return (jnp.where(last, 0, (i + 1) * (tm // halo)), 0)

        def edge_fn(i):
            _, last, _ = locate(i)
            return True, jnp.logical_not(last)

        tile_map = lambda i: (i, 0)
        w_map = lambda i: (layer, 0, 0)
        prefetch = ()

    tile = lambda: pl.BlockSpec((rows, d), tile_map)
    in_specs = [
        tile(), tile(), tile(), tile(),
        pl.BlockSpec((halo, d), left_map),
        pl.BlockSpec((halo, d), right_map),
        tile(), tile(),
        _resident((None, 3, d), w_map),
        _resident((None, 1, d), w_map),
        _resident((None, d, d), w_map),
        _resident((None, d, d), w_map),
        _resident((None, d, d), w_map),
    ]
    kernel = functools.partial(_merge_kernel, edge_fn=edge_fn)
    if meta:
        body = kernel
        kernel = lambda fb_ref, *refs: body(*refs)
    grid_spec = pltpu.PrefetchScalarGridSpec(
        num_scalar_prefetch=len(prefetch), grid=grid, in_specs=in_specs, out_specs=tile())
    return pl.pallas_call(
        kernel,
        out_shape=jax.ShapeDtypeStruct(h.shape, F32),
        grid_spec=grid_spec,
        input_output_aliases={len(prefetch): 0},
        compiler_params=_params("arbitrary"),
        name="merge_meta" if meta else "merge",
    )(*prefetch, h, a, cb, z, z, z, ga, gc, conv_w, conv_b, woa, woc, wm)


def _final_norm_kernel(x_ref, g_ref, o_ref):
    o_ref[...] = _rmsnorm_rows(x_ref[...], g_ref[...])


def _final_norm(lay, x, g):
    d, tm = lay.d, lay.tm
    return pl.pallas_call(
        _final_norm_kernel,
        out_shape=jax.ShapeDtypeStruct((lay.real, d), F32),
        grid=(lay.real // tm,),
        in_specs=[pl.BlockSpec((tm, d), lambda i: (i, 0)), _resident((1, d), lambda i: (0, 0))],
        out_specs=pl.BlockSpec((tm, d), lambda i: (i, 0)),
        compiler_params=_params("parallel"),
        name="final_norm",
    )(x, g)


def _rope_table(lay):
    rows, cols = [], []
    for b, n in lay.groups:
        t = jnp.arange(n, dtype=jnp.int32)
        rows.append(jnp.tile((t // GRID_W).astype(F32), b))
        cols.append(jnp.tile((t % GRID_W).astype(F32), b))
    pad = lay.total - lay.real - lay.nb * lay.n_meta
    rows += [jnp.full((lay.nb * lay.n_meta,), -1.0, F32), jnp.zeros((pad,), F32)]
    cols += [jnp.tile(jnp.arange(lay.n_meta, dtype=F32), lay.nb), jnp.zeros((pad,), F32)]
    row, col = jnp.concatenate(rows), jnp.concatenate(cols)
    half = lay.hd // 2
    inv_freq = ROPE_BASE ** (-jnp.arange(0, half, 2, dtype=F32) / half)
    ang_r = inv_freq[:, None] * row[None, :]
    ang_c = inv_freq[:, None] * col[None, :]
    return jnp.concatenate([jnp.cos(ang_r), jnp.sin(ang_r), jnp.cos(ang_c), jnp.sin(ang_c)], axis=0)


def kernel(x_prompt, x_sample, meta_tokens, ffn1_norm, ffn1_w_gate, ffn1_w_up, ffn1_w_down, mix_norm, w_in, conv_w, conv_b, q_norm, k_norm, w_o_attn, w_o_conv, w_merge, ffn2_norm, ffn2_w_gate, ffn2_w_up, ffn2_w_down, final_norm):
    x_groups = (x_prompt, x_sample)
    lay = _make_layout(x_groups, meta_tokens, q_norm, w_o_attn, w_in)
    d, hd = lay.d, lay.hd
    depth = w_in.shape[0]
    qkv_w = (lay.nq + 2 * lay.nkv) * hd

    pad = lay.total - lay.real - lay.nb * lay.n_meta
    h = jnp.concatenate([x.reshape(-1, d) for x in x_groups]
                        + [jnp.tile(meta_tokens.astype(F32), (lay.nb, 1)), jnp.zeros((pad, d), F32)], axis=0)

    bf = lambda w: w.astype(BF16)
    row3 = lambda g: g.reshape(depth, 1, -1)
    wqkv_t = bf(jnp.swapaxes(w_in[:, :, :qkv_w], 1, 2))
    wrest = bf(w_in[:, :, qkv_w:])
    qn = q_norm.reshape(depth, hd, 1)
    kn = k_norm.reshape(depth, hd, 1)
    f1 = (row3(ffn1_norm), bf(ffn1_w_gate), bf(ffn1_w_up), bf(ffn1_w_down))
    f2 = (row3(ffn2_norm), bf(ffn2_w_gate), bf(ffn2_w_up), bf(ffn2_w_down))
    mixn, cbias = row3(mix_norm), row3(conv_b)
    woa, woc, wm = bf(w_o_attn), bf(w_o_conv), bf(w_merge)
    rope = _rope_table(lay)

    for layer in range(depth):
        h = _ffn(lay, layer, h, *f1)
        qt, k, vt, cb, z, ga, gc = _inproj(lay, layer, h, mixn, wqkv_t, wrest, qn, kn, rope)
        a = None
        row0 = gb0 = 0
        for batch, n_tok in lay.groups:
            for meta_queries in (False, True):
                a = _attention(lay, qt, k, vt, a, row0=row0, gb0=gb0, batch=batch, n_tok=n_tok,
                               meta_queries=meta_queries)
            row0 += batch * n_tok
            gb0 += batch
        for meta in (False, True):
            h = _merge(lay, layer, h, a, cb, z, ga, gc, conv_w, cbias, woa, woc, wm, meta=meta)
        h = _ffn(lay, layer, h, *f2)

    y = _final_norm(lay, h, final_norm.reshape(1, d))
    outs = []
    row0 = 0
    for (batch, n_tok), x in zip(lay.groups, x_groups):
        outs.append(y[row0:row0 + batch * n_tok].reshape(x.shape))
        row0 += batch * n_tok
    return tuple(outs)
```
